```python
import jax, jax.numpy as jnp
from jax import lax
import numpy as np

D_MODEL = 1024
BATCH = 2
SEQ = 16384
DEPTH = 1

CHUNK = 64
SG_BLOCK = 128
A_WIDTH = 1024
A_GROUPS = 8
A_GROUP_DIM = A_WIDTH // A_GROUPS
B_WIDTH = 1024
B_GROUPS = 8
CONV_WIDTH = 3
D_FF = 4 * D_MODEL
N_BRANCHES = 2
EPS = 1e-6
IN_COLS = 2 * A_WIDTH + 3 * B_WIDTH + N_BRANCHES * D_MODEL

kernel_name = "hybrid_sgmlp_shortconv_gated_block"


def rmsnorm(x, g):
    xf = x.astype(jnp.float32)
    y = xf * lax.rsqrt(jnp.mean(xf * xf, axis=-1, keepdims=True) + EPS)
    return (y * g.astype(jnp.float32)).astype(x.dtype)


def chunk_mask():
    c = jnp.arange(SG_BLOCK) // CHUNK
    return c[None, :] <= c[:, None]


def spatial_gating(u, v, w_s, b_s):
    bsz, s, _ = v.shape
    vb = v.reshape(bsz, s // SG_BLOCK, SG_BLOCK, A_GROUPS, A_GROUP_DIM)
    w = jnp.where(chunk_mask()[None], w_s, jnp.zeros_like(w_s))
    mixed = jnp.einsum('gij,bnjgc->bnigc', w, vb) + b_s.T[None, None, :, :, None]
    return u * mixed.reshape(bsz, s, A_WIDTH)


def causal_dwconv(z, w):
    s = z.shape[1]
    zp = jnp.pad(z, ((0, 0), (CONV_WIDTH - 1, 0), (0, 0)))
    y = w[0] * zp[:, 0:s]
    for k in range(1, CONV_WIDTH):
        y = y + w[k] * zp[:, k:k + s]
    return y


def setup_inputs(seed: int = 0) -> dict:
    key = jax.random.key(seed)
    ks = jax.random.split(key, 16)
    f32 = jnp.float32
    nrm = lambda k, shape, scale: jax.random.normal(k, shape, f32) * scale
    return {
        "x": nrm(ks[0], (BATCH, SEQ, D_MODEL), 1.0),
        "norm_mix_g": 1.0 + nrm(ks[1], (DEPTH, D_MODEL), 0.1),
        "w_in": nrm(ks[2], (DEPTH, D_MODEL, IN_COLS), D_MODEL ** -0.5),
        "b_gate": nrm(ks[3], (DEPTH, N_BRANCHES * D_MODEL), 0.01),
        "norm_v_g": 1.0 + nrm(ks[4], (DEPTH, A_WIDTH), 0.1),
        "w_s": nrm(ks[5], (DEPTH, A_GROUPS, SG_BLOCK, SG_BLOCK), SG_BLOCK ** -0.5),
        "b_s": 1.0 + nrm(ks[6], (DEPTH, A_GROUPS, SG_BLOCK), 0.1),
        "conv_w": nrm(ks[7], (DEPTH, CONV_WIDTH, B_WIDTH), CONV_WIDTH ** -0.5),
        "w_proj_a": nrm(ks[8], (DEPTH, A_WIDTH, D_MODEL), A_WIDTH ** -0.5),
        "w_proj_b": nrm(ks[9], (DEPTH, B_WIDTH, D_MODEL), B_WIDTH ** -0.5),
        "w_out": nrm(ks[10], (DEPTH, D_MODEL, D_MODEL), D_MODEL ** -0.5),
        "norm_ff_g": 1.0 + nrm(ks[11], (DEPTH, D_MODEL), 0.1),
        "w_ff1": nrm(ks[12], (DEPTH, D_MODEL, D_FF), D_MODEL ** -0.5),
        "w_ff2": nrm(ks[13], (DEPTH, D_FF, D_MODEL), D_FF ** -0.5),
        "norm_final_g": 1.0 + nrm(ks[14], (D_MODEL,), 0.1),
    }


def reference(x, norm_mix_g, w_in, b_gate, norm_v_g, w_s, b_s, conv_w, w_proj_a, w_proj_b,
              w_out, norm_ff_g, w_ff1, w_ff2, norm_final_g):
    split_at = [A_WIDTH, 2 * A_WIDTH, 2 * A_WIDTH + B_WIDTH, 2 * A_WIDTH + 2 * B_WIDTH,
                2 * A_WIDTH + 3 * B_WIDTH, 2 * A_WIDTH + 3 * B_WIDTH + D_MODEL]
    for l in range(DEPTH):
        h = rmsnorm(x, norm_mix_g[l])
        proj = jnp.einsum('bsd,dc->bsc', h, w_in[l])
        u, v, bg, cg, xs, ga, gb = jnp.split(proj, split_at, axis=-1)
        ga = ga + b_gate[l, :D_MODEL]
        gb = gb + b_gate[l, D_MODEL:]

        u = jax.nn.gelu(u, approximate=False)
        v = rmsnorm(jax.nn.gelu(v, approximate=False), norm_v_g[l])
        a = spatial_gating(u, v, w_s[l], b_s[l])

        c = bg * causal_dwconv(cg * xs, conv_w[l])

        m = (jax.nn.sigmoid(ga) * jnp.einsum('bsc,cd->bsd', a, w_proj_a[l])
             + jax.nn.sigmoid(gb) * jnp.einsum('bsc,cd->bsd', c, w_proj_b[l]))
        x = x + jnp.einsum('bsd,de->bse', m, w_out[l])

        hf = rmsnorm(x, norm_ff_g[l])
        z = jax.nn.relu(jnp.einsum('bsd,df->bsf', hf, w_ff1[l]))
        x = x + jnp.einsum('bsf,fd->bsd', z * z, w_ff2[l])
    return rmsnorm(x, norm_final_g)
```

```python
import functools

import jax
import jax.numpy as jnp
from jax import lax
from jax.experimental import pallas as pl
from jax.experimental.pallas import tpu as pltpu

D_MODEL = 1024
A_WIDTH = 1024
A_GROUPS = 8
B_WIDTH = 1024
SG_BLOCK = 128
CHUNK = 64
CONV_WIDTH = 3
D_FF = 4 * D_MODEL
EPS = 1e-6

SUBLANES = 8
MIXER_TILE = 512
FFN_TILE = 512
FF_CHUNK = 1024
VMEM_LIMIT_BYTES = 56 * 1024 * 1024

_U0 = 0
_V0 = A_WIDTH
_BG0 = 2 * A_WIDTH
_CG0 = _BG0 + B_WIDTH
_XS0 = _CG0 + B_WIDTH
_GA0 = _XS0 + B_WIDTH
_GB0 = _GA0 + D_MODEL


def _rmsnorm(x, g):
    return x * lax.rsqrt(jnp.mean(x * x, axis=-1, keepdims=True) + EPS) * g


def _gelu(x):
    return 0.5 * x * (1.0 + lax.erf(x * (2.0 ** -0.5)))


def _dot(a, b):
    return jnp.dot(a, b, preferred_element_type=jnp.float32)


def _mixer_kernel(x_ref, g_mix_ref, w_in_ref, b_gate_ref, g_v_ref, w_s_ref, b_s_ref,
                  conv_w_ref, w_pa_ref, w_pb_ref, w_o_ref, o_ref, z_ref, a_ref,
                  *, tiles_per_seq):
    tm = x_ref.shape[0]
    x = x_ref[...]
    h = _rmsnorm(x, g_mix_ref[...]).astype(jnp.bfloat16)

    def proj(c0, width):
        return _dot(h, w_in_ref[:, c0:c0 + width])

    v = _rmsnorm(_gelu(proj(_V0, A_WIDTH)), g_v_ref[...]).astype(jnp.bfloat16)
    u = _gelu(proj(_U0, A_WIDTH))
    chunk_row = lax.broadcasted_iota(jnp.int32, (SG_BLOCK, SG_BLOCK), 0) // CHUNK
    chunk_col = lax.broadcasted_iota(jnp.int32, (SG_BLOCK, SG_BLOCK), 1) // CHUNK
    keep = chunk_col <= chunk_row
    for g in range(A_GROUPS):
        w_g = jnp.where(keep, w_s_ref[g], 0.0).astype(jnp.bfloat16)
        b_g = b_s_ref[g]
        cols = slice(g * SG_BLOCK, (g + 1) * SG_BLOCK)
        for n in range(tm // SG_BLOCK):
            rows = slice(n * SG_BLOCK, (n + 1) * SG_BLOCK)
            mixed = _dot(w_g, v[rows, cols]) + b_g
            a_ref[rows, cols] = (u[rows, cols] * mixed).astype(jnp.bfloat16)

    @pl.when(pl.program_id(0) % tiles_per_seq == 0)
    def _():
        z_ref[0:SUBLANES, :] = jnp.zeros((SUBLANES, B_WIDTH), jnp.float32)

    z = proj(_CG0, B_WIDTH) * proj(_XS0, B_WIDTH)
    z_ref[SUBLANES:SUBLANES + tm, :] = z
    conv = conv_w_ref[CONV_WIDTH - 1:CONV_WIDTH, :] * z
    for k in range(CONV_WIDTH - 1):
        shift = CONV_WIDTH - 1 - k
        conv = conv + conv_w_ref[k:k + 1, :] * z_ref[SUBLANES - shift:SUBLANES - shift + tm, :]
    z_ref[0:SUBLANES, :] = z_ref[tm:tm + SUBLANES, :]
    c = (proj(_BG0, B_WIDTH) * conv).astype(jnp.bfloat16)

    ga = proj(_GA0, D_MODEL) + b_gate_ref[:, 0:D_MODEL]
    gb = proj(_GB0, D_MODEL) + b_gate_ref[:, D_MODEL:2 * D_MODEL]
    m = (jax.nn.sigmoid(ga) * _dot(a_ref[...], w_pa_ref[...])
         + jax.nn.sigmoid(gb) * _dot(c, w_pb_ref[...]))
    o_ref[...] = x + _dot(m.astype(jnp.bfloat16), w_o_ref[...])


def _ffn_kernel(x_ref, g_ff_ref, w1_ref, w2_ref, g_fin_ref, o_ref, *, final_norm):
    x = x_ref[...]
    hf = _rmsnorm(x, g_ff_ref[...]).astype(jnp.bfloat16)
    acc = x
    for j in range(D_FF // FF_CHUNK):
        cols = slice(j * FF_CHUNK, (j + 1) * FF_CHUNK)
        z = jnp.maximum(_dot(hf, w1_ref[:, cols]), 0.0)
        acc = acc + _dot((z * z).astype(jnp.bfloat16), w2_ref[cols, :])
    o_ref[...] = _rmsnorm(acc, g_fin_ref[...]) if final_norm else acc


def _resident(shape):
    return pl.BlockSpec(shape, lambda i: (0,) * len(shape), pipeline_mode=pl.Buffered(1))


def kernel(x, norm_mix_g, w_in, b_gate, norm_v_g, w_s, b_s, conv_w, w_proj_a, w_proj_b,
           w_out, norm_ff_g, w_ff1, w_ff2, norm_final_g):
    bsz, seq, d = x.shape
    depth = w_in.shape[0]
    assert d == D_MODEL and seq % MIXER_TILE == 0 and seq % FFN_TILE == 0
    tokens = bsz * seq
    bf16 = jnp.bfloat16
    xt = x.reshape(tokens, d)
    params = pltpu.CompilerParams(dimension_semantics=("arbitrary",),
                                  vmem_limit_bytes=VMEM_LIMIT_BYTES)

    for l in range(depth):
        xt = pl.pallas_call(
            functools.partial(_mixer_kernel, tiles_per_seq=seq // MIXER_TILE),
            grid=(tokens // MIXER_TILE,),
            in_specs=[
                pl.BlockSpec((MIXER_TILE, d), lambda i: (i, 0)),
                _resident((1, d)),
                _resident(w_in.shape[1:]),
                _resident((1, 2 * d)),
                _resident((1, A_WIDTH)),
                _resident(w_s.shape[1:]),
                _resident((A_GROUPS, SG_BLOCK, 1)),
                _resident(conv_w.shape[1:]),
                _resident(w_proj_a.shape[1:]),
                _resident(w_proj_b.shape[1:]),
                _resident(w_out.shape[1:]),
            ],
            out_specs=pl.BlockSpec((MIXER_TILE, d), lambda i: (i, 0)),
            out_shape=jax.ShapeDtypeStruct((tokens, d), jnp.float32),
            scratch_shapes=[
                pltpu.VMEM((MIXER_TILE + SUBLANES, B_WIDTH), jnp.float32),
                pltpu.VMEM((MIXER_TILE, A_WIDTH), bf16),
            ],
            compiler_params=params,
            name="mixer",
        )(xt, norm_mix_g[l][None], w_in[l].astype(bf16), b_gate[l][None], norm_v_g[l][None],
          w_s[l], b_s[l][:, :, None], conv_w[l], w_proj_a[l].astype(bf16),
          w_proj_b[l].astype(bf16), w_out[l].astype(bf16))

        xt = pl.pallas_call(
            functools.partial(_ffn_kernel, final_norm=(l == depth - 1)),
            grid=(tokens // FFN_TILE,),
            in_specs=[
                pl.BlockSpec((FFN_TILE, d), lambda i: (i, 0)),
                _resident((1, d)),
                _resident(w_ff1.shape[1:]),
                _resident(w_ff2.shape[1:]),
                _resident((1, d)),
            ],
            out_specs=pl.BlockSpec((FFN_TILE, d), lambda i: (i, 0)),
            out_shape=jax.ShapeDtypeStruct((tokens, d), jnp.float32),
            compiler_params=params,
            name="ffn",
        )(xt, norm_ff_g[l][None], w_ff1[l].astype(bf16), w_ff2[l].astype(bf16), norm_final_g[None])

    return xt.reshape(bsz, seq, d)
```

```python
import functools

import jax
import jax.numpy as jnp
from jax import lax
from jax.experimental import pallas as pl
from jax.experimental.pallas import tpu as pltpu

D_MODEL = 1024
A_WIDTH = 1024
A_GROUPS = 8
B_WIDTH = 1024
SG_BLOCK = 128
CHUNK = 64
CONV_WIDTH = 3
D_FF = 4 * D_MODEL
EPS = 1e-6

SUBLANES = 8
MIXER_TILE = 512
FFN_TILE = 1024
FF_CHUNK = 1024
FFN_ROW_BLOCK = 256
VMEM_LIMIT_BYTES =56 * 1024 * 1024

_U0 = 0
_V0 = A_WIDTH
_BG0 = 2 * A_WIDTH
_CG0 = _BG0 + B_WIDTH
_XS0 = _CG0 + B_WIDTH
_GA0 = _XS0 + B_WIDTH
_GB0 = _GA0 + D_MODEL


def _rmsnorm(x, g):
    return x * lax.rsqrt(jnp.mean(x * x, axis=-1, keepdims=True) + EPS) * g


def _gelu(x):
    return 0.5 * x * (1.0 + lax.erf(x * (2.0 ** -0.5)))


def _sigmoid(x):
    return 0.5 * jnp.tanh(0.5 * x) + 0.5


def _dot(a, b):
    return jnp.dot(a, b, preferred_element_type=jnp.float32)


def _mixer_kernel(x_ref, g_mix_ref, w_in_ref, b_gate_ref, g_v_ref, w_s_ref, b_s_ref,
                  conv_w_ref, w_pa_ref, w_pb_ref, w_o_ref, w1_f32_ref, w2_f32_ref,
                  o_ref, w1_bf16_ref, w2_bf16_ref, z_ref, a_ref, *, tiles_per_seq):
    tm = x_ref.shape[0]
    x = x_ref[...]
    h = _rmsnorm(x, g_mix_ref[...]).astype(jnp.bfloat16)

    def proj(c0, width):
        return _dot(h, w_in_ref[:, c0:c0 + width])

    def proj_pair(heavy0, light0):
        half = A_WIDTH // 2
        parts = [proj(c0 + k * half, half) for k in range(2) for c0 in (heavy0, light0)]
        return (jnp.concatenate(parts[0::2], axis=-1), jnp.concatenate(parts[1::2], axis=-1))

    pv, pcg = proj_pair(_V0, _CG0)
    pu, pxs = proj_pair(_U0, _XS0)
    v = _rmsnorm(_gelu(pv), g_v_ref[...]).astype(jnp.bfloat16)
    u = _gelu(pu)

    @pl.when(pl.program_id(0) % tiles_per_seq == 0)
    def _():
        z_ref[0:SUBLANES, :] = jnp.zeros((SUBLANES, B_WIDTH), jnp.float32)

    z = pcg * pxs
    z_ref[SUBLANES:SUBLANES + tm, :] = z
    conv = conv_w_ref[CONV_WIDTH - 1:CONV_WIDTH, :] * z
    for k in range(CONV_WIDTH - 1):
        shift = CONV_WIDTH - 1 - k
        conv = conv + conv_w_ref[k:k + 1, :] * z_ref[SUBLANES - shift:SUBLANES - shift + tm, :]
    z_ref[0:SUBLANES, :] = z_ref[tm:tm + SUBLANES, :]
    c = (proj(_BG0, B_WIDTH) * conv).astype(jnp.bfloat16)
    gb = proj(_GB0, D_MODEL) + b_gate_ref[:, D_MODEL:2 * D_MODEL]
    m_b = _sigmoid(gb) * _dot(c, w_pb_ref[...])

    chunk_row = lax.broadcasted_iota(jnp.int32, (SG_BLOCK, SG_BLOCK), 0) // CHUNK
    chunk_col = lax.broadcasted_iota(jnp.int32, (SG_BLOCK, SG_BLOCK), 1) // CHUNK
    keep = chunk_col <= chunk_row
    for g in range(A_GROUPS):
        w_g = jnp.where(keep, w_s_ref[g], 0.0).astype(jnp.bfloat16)
        b_g = b_s_ref[g]
        cols = slice(g * SG_BLOCK, (g + 1) * SG_BLOCK)
        for n in range(tm // SG_BLOCK):
            rows = slice(n * SG_BLOCK, (n + 1) * SG_BLOCK)
            mixed = _dot(w_g, v[rows, cols]) + b_g
            a_ref[rows, cols] = (u[rows, cols] * mixed).astype(jnp.bfloat16)

    ga = proj(_GA0, D_MODEL) + b_gate_ref[:, 0:D_MODEL]
    m = _sigmoid(ga) * _dot(a_ref[...], w_pa_ref[...]) + m_b
    o_ref[...] = x + _dot(m.astype(jnp.bfloat16), w_o_ref[...])

    w1_bf16_ref[...] = w1_f32_ref[...].astype(jnp.bfloat16)
    w2_bf16_ref[...] = w2_f32_ref[...].astype(jnp.bfloat16)


def _ffn_kernel(x_ref, g_ff_ref, w1_ref, w2_ref, g_fin_ref, o_ref, *, final_norm):
    x = x_ref[...]
    hf = _rmsnorm(x, g_ff_ref[...]).astype(jnp.bfloat16)
    acc = x
    n_chunks = D_FF // FF_CHUNK
    for j in range(n_chunks):
        cols = slice(j * FF_CHUNK, (j + 1) * FF_CHUNK)
        z = jnp.maximum(_dot(hf, w1_ref[:, cols]), 0.0)
        zz = (z * z).astype(jnp.bfloat16)
        if j < n_chunks - 1:
            acc = acc + _dot(zz, w2_ref[cols, :])
    for r in range(x.shape[0] // FFN_ROW_BLOCK):
        rows = slice(r * FFN_ROW_BLOCK, (r + 1) * FFN_ROW_BLOCK)
        out = acc[rows] + _dot(zz[rows], w2_ref[cols, :])
        o_ref[rows, :] = _rmsnorm(out, g_fin_ref[...]) if final_norm else out


def _resident(shape):
    return pl.BlockSpec(shape, lambda i: (0,) * len(shape), pipeline_mode=pl.Buffered(1))


def _row_slab(rows, cols):
    return pl.BlockSpec((rows, cols), lambda i: (i, 0))


def kernel(x, norm_mix_g, w_in, b_gate, norm_v_g, w_s, b_s, conv_w, w_proj_a, w_proj_b,
           w_out, norm_ff_g, w_ff1, w_ff2, norm_final_g):
    bsz, seq, d = x.shape
    depth = w_in.shape[0]
    assert d == D_MODEL and seq % MIXER_TILE == 0 and seq % FFN_TILE == 0
    tokens = bsz * seq
    bf16 = jnp.bfloat16
    xt = x.reshape(tokens, d)
    params = pltpu.CompilerParams(dimension_semantics=("arbitrary",),
                                  vmem_limit_bytes=VMEM_LIMIT_BYTES)
    mixer_steps = tokens // MIXER_TILE
    w1_slab = d // mixer_steps
    w2_slab = D_FF // mixer_steps
    assert w1_slab * mixer_steps == d and w1_slab % (2 * SUBLANES) == 0

    for l in range(depth):
        xt, w1_bf16, w2_bf16 = pl.pallas_call(
            functools.partial(_mixer_kernel, tiles_per_seq=seq // MIXER_TILE),
            grid=(mixer_steps,),
            in_specs=[
                _row_slab(MIXER_TILE, d),
                _resident((1, d)),
                _resident(w_in.shape[1:]),
                _resident((1, 2 * d)),
                _resident((1, A_WIDTH)),
                _resident(w_s.shape[1:]),
                _resident((A_GROUPS, SG_BLOCK, 1)),
                _resident(conv_w.shape[1:]),
                _resident(w_proj_a.shape[1:]),
                _resident(w_proj_b.shape[1:]),
                _resident(w_out.shape[1:]),
                _row_slab(w1_slab, D_FF),
                _row_slab(w2_slab, d),
            ],
            out_specs=[
                _row_slab(MIXER_TILE, d),
                _row_slab(w1_slab, D_FF),
                _row_slab(w2_slab, d),
            ],
            out_shape=[
                jax.ShapeDtypeStruct((tokens, d), jnp.float32),
                jax.ShapeDtypeStruct((d, D_FF), bf16),
                jax.ShapeDtypeStruct((D_FF, d), bf16),
            ],
            scratch_shapes=[
                pltpu.VMEM((MIXER_TILE + SUBLANES, B_WIDTH), jnp.float32),
                pltpu.VMEM((MIXER_TILE, A_WIDTH), bf16),
            ],
            compiler_params=params,
            name="mixer",
        )(xt, norm_mix_g[l][None], w_in[l].astype(bf16), b_gate[l][None], norm_v_g[l][None],
          w_s[l], b_s[l][:, :, None], conv_w[l], w_proj_a[l].astype(bf16),
          w_proj_b[l].astype(bf16), w_out[l].astype(bf16), w_ff1[l], w_ff2[l])

        xt = pl.pallas_call(
            functools.partial(_ffn_kernel, final_norm=(l == depth - 1)),
            grid=(tokens // FFN_TILE,),
            in_specs=[
                _row_slab(FFN_TILE, d),
                _resident((1, d)),
                _resident((d, D_FF)),
                _resident((D_FF, d)),
                _resident((1, d)),
            ],
            out_specs=_row_slab(FFN_TILE, d),
            out_shape=jax.ShapeDtypeStruct((tokens, d), jnp.float32),
            compiler_params=params,
            name="ffn",
        )(xt, norm_ff_g[l][None], w1_bf16, w2_bf16, norm_final_g[None])

    return xt.reshape(bsz, seq, d)
```

```python
import functools

import jax
import jax.numpy as jnp
from jax import lax
from jax.experimental import pallas as pl
from jax.experimental.pallas import tpu as pltpu

D_MODEL = 1024
A_WIDTH = 1024
A_GROUPS = 8
B_WIDTH = 1024
SG_BLOCK = 128
CHUNK = 64
CONV_WIDTH = 3
D_FF = 4 * D_MODEL
EPS = 1e-6

SUBLANES = 8
MIXER_TILE = 512
FFN_TILE = 1024
FF_CHUNK = 1024
FFN_ROW_BLOCK = 256
WEIGHT_CHUNK = 512
WEIGHT_STAGES = 3
VMEM_LIMIT_BYTES = 56 * 1024 * 1024

_U0 = 0
_V0 = A_WIDTH
_BG0 = 2 * A_WIDTH
_CG0 = _BG0 + B_WIDTH
_XS0 = _CG0 + B_WIDTH
_GA0 = _XS0 + B_WIDTH
_GB0 = _GA0 + D_MODEL


def _rmsnorm(x, g):
    return x * lax.rsqrt(jnp.mean(x * x, axis=-1, keepdims=True) + EPS) * g


def _gelu(x):
    return 0.5 * x * (1.0 + lax.erf(x * (2.0 ** -0.5)))


def _sigmoid(x):
    return 0.5 * jnp.tanh(0.5 * x) + 0.5


def _dot(a, b):
    return jnp.dot(a, b, preferred_element_type=jnp.float32)


def _load_weights_as_bf16(layer, hbm_and_vmem, stage_ref, sem_ref):
    chunks = [(src, dst, c0) for src, dst in hbm_and_vmem
              for c0 in range(0, dst.shape[1], WEIGHT_CHUNK)]

    def copy(i):
        src, _, c0 = chunks[i]
        slot = i % WEIGHT_STAGES
        return pltpu.make_async_copy(src.at[layer, :, pl.ds(c0, WEIGHT_CHUNK)],
                                     stage_ref.at[slot], sem_ref.at[slot])

    for i in range(min(WEIGHT_STAGES, len(chunks))):
        copy(i).start()
    for i, (_, dst, c0) in enumerate(chunks):
        copy(i).wait()
        dst[:, c0:c0 + WEIGHT_CHUNK] = stage_ref[i % WEIGHT_STAGES].astype(jnp.bfloat16)
        if i + WEIGHT_STAGES < len(chunks):
            copy(i + WEIGHT_STAGES).start()


def _mixer_kernel(x_ref, g_mix_ref, w_in_hbm, b_gate_ref, g_v_ref, w_s_ref, b_s_ref,
                  conv_w_ref, w_pa_hbm, w_pb_hbm, w_o_hbm, w1_f32_ref, w2_f32_ref,
                  o_ref, w1_bf16_ref, w2_bf16_ref,
                  w_in_ref, w_pa_ref, w_pb_ref, w_o_ref, stage_ref, sem_ref, z_ref, a_ref,
                  *, layer, tiles_per_seq):
    @pl.when(pl.program_id(0) == 0)
    def _():
        _load_weights_as_bf16(layer, [(w_in_hbm, w_in_ref), (w_pa_hbm, w_pa_ref),
                                      (w_pb_hbm, w_pb_ref), (w_o_hbm, w_o_ref)],
                              stage_ref, sem_ref)

    tm = x_ref.shape[0]
    x = x_ref[...]
    h = _rmsnorm(x, g_mix_ref[...]).astype(jnp.bfloat16)

    def proj(c0, width):
        return _dot(h, w_in_ref[:, c0:c0 + width])

    def proj_pair(heavy0, light0):
        half = A_WIDTH // 2
        parts = [proj(c0 + k * half, half) for k in range(2) for c0 in (heavy0, light0)]
        return (jnp.concatenate(parts[0::2], axis=-1), jnp.concatenate(parts[1::2], axis=-1))

    pv, pcg = proj_pair(_V0, _CG0)
    pu, pxs = proj_pair(_U0, _XS0)
    v = _rmsnorm(_gelu(pv), g_v_ref[...]).astype(jnp.bfloat16)
    u = _gelu(pu)

    @pl.when(pl.program_id(0) % tiles_per_seq == 0)
    def _():
        z_ref[0:SUBLANES, :] = jnp.zeros((SUBLANES, B_WIDTH), jnp.float32)

    z = pcg * pxs
    z_ref[SUBLANES:SUBLANES + tm, :] = z
    conv = conv_w_ref[CONV_WIDTH - 1:CONV_WIDTH, :] * z
    for k in range(CONV_WIDTH - 1):
        shift = CONV_WIDTH - 1 - k
        conv = conv + conv_w_ref[k:k + 1, :] * z_ref[SUBLANES - shift:SUBLANES - shift + tm, :]
    z_ref[0:SUBLANES, :] = z_ref[tm:tm + SUBLANES, :]
    c = (proj(_BG0, B_WIDTH) * conv).astype(jnp.bfloat16)
    gb = proj(_GB0, D_MODEL) + b_gate_ref[:, D_MODEL:2 * D_MODEL]
    m_b = _sigmoid(gb) * _dot(c, w_pb_ref[...])

    chunk_row = lax.broadcasted_iota(jnp.int32, (SG_BLOCK, SG_BLOCK), 0) // CHUNK
    chunk_col = lax.broadcasted_iota(jnp.int32, (SG_BLOCK, SG_BLOCK), 1) // CHUNK
    keep = chunk_col <= chunk_row
    for g in range(A_GROUPS):
        w_g = jnp.where(keep, w_s_ref[g], 0.0).astype(jnp.bfloat16)
        b_g = b_s_ref[g]
        cols = slice(g * SG_BLOCK, (g + 1) * SG_BLOCK)
        for n in range(tm // SG_BLOCK):
            rows = slice(n * SG_BLOCK, (n + 1) * SG_BLOCK)
            mixed = _dot(w_g, v[rows, cols]) + b_g
            a_ref[rows, cols] = (u[rows, cols] * mixed).astype(jnp.bfloat16)

    ga = proj(_GA0, D_MODEL) + b_gate_ref[:, 0:D_MODEL]
    m = _sigmoid(ga) * _dot(a_ref[...], w_pa_ref[...]) + m_b
    o_ref[...] = x + _dot(m.astype(jnp.bfloat16), w_o_ref[...])

    w1_bf16_ref[...] = w1_f32_ref[...].astype(jnp.bfloat16)
    w2_bf16_ref[...] = w2_f32_ref[...].astype(jnp.bfloat16)


def _ffn_kernel(x_ref, g_ff_ref, w1_ref, w2_ref, g_fin_ref, o_ref, *, final_norm):
    x = x_ref[...]
    hf = _rmsnorm(x, g_ff_ref[...]).astype(jnp.bfloat16)
    acc = x
    n_chunks = D_FF // FF_CHUNK
    for j in range(n_chunks):
        cols = slice(j * FF_CHUNK, (j + 1) * FF_CHUNK)
        z = jnp.maximum(_dot(hf, w1_ref[:, cols]), 0.0)
        zz = (z * z).astype(jnp.bfloat16)
        if j < n_chunks - 1:
            acc = acc + _dot(zz, w2_ref[cols, :])
    for r in range(x.shape[0] // FFN_ROW_BLOCK):
        rows = slice(r * FFN_ROW_BLOCK, (r + 1) * FFN_ROW_BLOCK)
        out = acc[rows] + _dot(zz[rows], w2_ref[cols, :])
        o_ref[rows, :] = _rmsnorm(out, g_fin_ref[...]) if final_norm else out


def _resident(shape):
    return pl.BlockSpec(shape, lambda i: (0,) * len(shape), pipeline_mode=pl.Buffered(1))


def _row_slab(rows, cols):
    return pl.BlockSpec((rows, cols), lambda i: (i, 0))


def kernel(x, norm_mix_g, w_in, b_gate, norm_v_g, w_s, b_s, conv_w, w_proj_a, w_proj_b,
           w_out, norm_ff_g, w_ff1, w_ff2, norm_final_g):
    bsz, seq, d = x.shape
    depth = w_in.shape[0]
    assert d == D_MODEL and seq % MIXER_TILE == 0 and seq % FFN_TILE == 0
    tokens = bsz * seq
    bf16 = jnp.bfloat16
    xt = x.reshape(tokens, d)
    params = pltpu.CompilerParams(dimension_semantics=("arbitrary",),
                                  vmem_limit_bytes=VMEM_LIMIT_BYTES)
    hbm = pl.BlockSpec(memory_space=pl.ANY)
    mixer_steps = tokens // MIXER_TILE
    w1_slab = d // mixer_steps
    w2_slab = D_FF // mixer_steps
    assert w1_slab * mixer_steps == d and w1_slab % (2 * SUBLANES) == 0

    for l in range(depth):
        xt, w1_bf16, w2_bf16 = pl.pallas_call(
            functools.partial(_mixer_kernel, layer=l, tiles_per_seq=seq // MIXER_TILE),
            grid=(mixer_steps,),
            in_specs=[
                _row_slab(MIXER_TILE, d),
                _resident((1, d)),
                hbm,
                _resident((1, 2 * d)),
                _resident((1, A_WIDTH)),
                _resident(w_s.shape[1:]),
                _resident((A_GROUPS, SG_BLOCK, 1)),
                _resident(conv_w.shape[1:]),
                hbm,
                hbm,
                hbm,
                _row_slab(w1_slab, D_FF),
                _row_slab(w2_slab, d),
            ],
            out_specs=[
                _row_slab(MIXER_TILE, d),
                _row_slab(w1_slab, D_FF),
                _row_slab(w2_slab, d),
            ],
            out_shape=[
                jax.ShapeDtypeStruct((tokens, d), jnp.float32),
                jax.ShapeDtypeStruct((d, D_FF), bf16),
                jax.ShapeDtypeStruct((D_FF, d), bf16),
            ],
            scratch_shapes=[
                pltpu.VMEM(w_in.shape[1:], bf16),
                pltpu.VMEM(w_proj_a.shape[1:], bf16),
                pltpu.VMEM(w_proj_b.shape[1:], bf16),
                pltpu.VMEM(w_out.shape[1:], bf16),
                pltpu.VMEM((WEIGHT_STAGES, d, WEIGHT_CHUNK), jnp.float32),
                pltpu.SemaphoreType.DMA((WEIGHT_STAGES,)),
                pltpu.VMEM((MIXER_TILE + SUBLANES, B_WIDTH), jnp.float32),
                pltpu.VMEM((MIXER_TILE, A_WIDTH), bf16),
            ],
            compiler_params=params,
            name="mixer",
        )(xt, norm_mix_g[l][None], w_in, b_gate[l][None], norm_v_g[l][None],
          w_s[l], b_s[l][:, :, None], conv_w[l], w_proj_a, w_proj_b, w_out, w_ff1[l], w_ff2[l])

        xt = pl.pallas_call(
            functools.partial(_ffn_kernel, final_norm=(l == depth - 1)),
            grid=(tokens // FFN_TILE,),
            in_specs=[
                _row_slab(FFN_TILE, d),
                _resident((1, d)),
                _resident((d, D_FF)),
                _resident((D_FF, d)),
                _resident((1, d)),
            ],
            out_specs=_row_slab(FFN_TILE, d),
            out_shape=jax.ShapeDtypeStruct((tokens, d), jnp.float32),
            compiler_params=params,
            name="ffn",
        )(xt, norm_ff_g[l][None], w1_bf16, w2_bf16, norm_final_g[None])

    return xt.reshape(bsz, seq, d)
```

```python
import functools

import jax
import jax.numpy as jnp
from jax import lax
from jax.experimental import pallas as pl
from jax.experimental.pallas import tpu as pltpu

D_MODEL = 1024
A_WIDTH = 1024
A_GROUPS = 8
B_WIDTH = 1024
SG_BLOCK = 128
CHUNK = 64
CONV_WIDTH = 3
D_FF = 4 * D_MODEL
EPS = 1e-6

SUBLANES = 8
MIXER_TILE = 512
FFN_TILE = 1024
FF_CHUNK = 1024
FFN_ROW_BLOCK = 256
WEIGHT_CHUNK = 512
WEIGHT_STAGES = 3
VMEM_LIMIT_BYTES = 56 * 1024 * 1024

_U0 = 0
_V0 = A_WIDTH
_BG0 = 2 * A_WIDTH
_CG0 = _BG0 + B_WIDTH
_XS0 = _CG0 + B_WIDTH
_GA0 = _XS0 + B_WIDTH
_GB0 = _GA0 + D_MODEL


def _rmsnorm(x, g):
    return x * lax.rsqrt(jnp.mean(x * x, axis=-1, keepdims=True) + EPS) * g


def _gelu(x):
    return 0.5 * x * (1.0 + lax.erf(x * (2.0 ** -0.5)))


def _sigmoid(x):
    return 0.5 * jnp.tanh(0.5 * x) + 0.5


def _dot(a, b):
    return jnp.dot(a, b, preferred_element_type=jnp.float32)


def _load_weights_as_bf16(layer, hbm_and_vmem, stage_ref, sem_ref):
    chunks = [(src, dst, c0) for src, dst in hbm_and_vmem
              for c0 in range(0, dst.shape[1], WEIGHT_CHUNK)]

    def copy(i):
        src, _, c0 = chunks[i]
        slot = i % WEIGHT_STAGES
        return pltpu.make_async_copy(src.at[layer, :, pl.ds(c0, WEIGHT_CHUNK)],
                                     stage_ref.at[slot], sem_ref.at[slot])

    for i in range(min(WEIGHT_STAGES, len(chunks))):
        copy(i).start()
    for i, (_, dst, c0) in enumerate(chunks):
        copy(i).wait()
        dst[:, c0:c0 + WEIGHT_CHUNK] = stage_ref[i % WEIGHT_STAGES].astype(jnp.bfloat16)
        if i + WEIGHT_STAGES < len(chunks):
            copy(i + WEIGHT_STAGES).start()


def _mixer_kernel(x_ref, g_mix_ref, w_in_hbm, b_gate_ref, g_v_ref, w_s_ref, b_s_ref,
                  conv_w_ref, w_pa_hbm, w_pb_hbm, w_o_hbm, w1_f32_ref, w2_f32_ref,
                  o_ref, w1_bf16_ref, w2_bf16_ref,
                  w_in_ref, w_pa_ref, w_pb_ref, w_o_ref, stage_ref, sem_ref, z_ref, a_ref,
                  *, layer, tiles_per_seq):
    @pl.when(pl.program_id(0) == 0)
    def _():
        z_ref[0:SUBLANES, :] = jnp.zeros((SUBLANES, B_WIDTH), jnp.float32)
        _load_weights_as_bf16(layer, [(w_in_hbm, w_in_ref), (w_pa_hbm, w_pa_ref),
                                      (w_pb_hbm, w_pb_ref), (w_o_hbm, w_o_ref)],
                              stage_ref, sem_ref)


    tm = x_ref.shape[0]
    x = x_ref[...]
    h = _rmsnorm(x, g_mix_ref[...]).astype(jnp.bfloat16)

    def proj(c0, width):
        return _dot(h, w_in_ref[:, c0:c0 + width])

    def proj_pair(heavy0, light0):
        half = A_WIDTH // 2
        parts = [proj(c0 + k * half, half) for k in range(2) for c0 in (heavy0, light0)]
        return (jnp.concatenate(parts[0::2], axis=-1), jnp.concatenate(parts[1::2], axis=-1))

    pv, pcg = proj_pair(_V0, _CG0)
    pu, pxs = proj_pair(_U0, _XS0)
    v = _rmsnorm(_gelu(pv), g_v_ref[...]).astype(jnp.bfloat16)
    u = _gelu(pu)

    z = pcg * pxs
    z_ref[SUBLANES:SUBLANES + tm, :] = z
    conv = conv_w_ref[CONV_WIDTH - 1:CONV_WIDTH, :] * z
    for k in range(CONV_WIDTH - 1):
        shift = CONV_WIDTH - 1 - k
        conv = conv + conv_w_ref[k:k + 1, :] * z_ref[SUBLANES - shift:SUBLANES - shift + tm, :]
    next_starts_seq = (pl.program_id(0) + 1) % tiles_per_seq == 0
    z_ref[0:SUBLANES, :] = jnp.where(next_starts_seq, 0.0, z_ref[tm:tm + SUBLANES, :])
    c = (proj(_BG0, B_WIDTH) * conv).astype(jnp.bfloat16)
    gb = proj(_GB0, D_MODEL) + b_gate_ref[:, D_MODEL:2 * D_MODEL]
    m_b = _sigmoid(gb) * _dot(c, w_pb_ref[...])

    chunk_row = lax.broadcasted_iota(jnp.int32, (SG_BLOCK, SG_BLOCK), 0) // CHUNK
    chunk_col = lax.broadcasted_iota(jnp.int32, (SG_BLOCK, SG_BLOCK), 1) // CHUNK
    keep = chunk_col <= chunk_row
    for g in range(A_GROUPS):
        w_g = jnp.where(keep, w_s_ref[g], 0.0).astype(jnp.bfloat16)
        b_g = b_s_ref[g]
        cols = slice(g * SG_BLOCK, (g + 1) * SG_BLOCK)
        n_blocks = tm // SG_BLOCK
        v_blocks = jnp.concatenate(
            [v[n * SG_BLOCK:(n + 1) * SG_BLOCK, cols] for n in range(n_blocks)], axis=1)
        mixed = _dot(w_g, v_blocks) + b_g
        for n in range(n_blocks):
            rows = slice(n * SG_BLOCK, (n + 1) * SG_BLOCK)
            a_ref[rows, cols] = (u[rows, cols]
                                 * mixed[:, n * SG_BLOCK:(n + 1) * SG_BLOCK]).astype(jnp.bfloat16)

    ga = proj(_GA0, D_MODEL) + b_gate_ref[:, 0:D_MODEL]
    m = _sigmoid(ga) * _dot(a_ref[...], w_pa_ref[...]) + m_b
    o_ref[...] = x + _dot(m.astype(jnp.bfloat16), w_o_ref[...])

    w1_bf16_ref[...] = w1_f32_ref[...].astype(jnp.bfloat16)
    w2_bf16_ref[...] = w2_f32_ref[...].astype(jnp.bfloat16)


def _ffn_kernel(x_ref, g_ff_ref, w1_ref, w2_ref, g_fin_ref, o_ref, *, final_norm):
    x = x_ref[...]
    hf = _rmsnorm(x, g_ff_ref[...]).astype(jnp.bfloat16)
    acc = x
    n_chunks = D_FF // FF_CHUNK
    for j in range(n_chunks):
        cols = slice(j * FF_CHUNK, (j + 1) * FF_CHUNK)
        z = jnp.maximum(_dot(hf, w1_ref[:, cols]), 0.0)
        zz = (z * z).astype(jnp.bfloat16)
        if j < n_chunks - 1:
            acc = acc + _dot(zz, w2_ref[cols, :])
    for r in range(x.shape[0] // FFN_ROW_BLOCK):
        rows = slice(r * FFN_ROW_BLOCK, (r + 1) * FFN_ROW_BLOCK)
        out = acc[rows] + _dot(zz[rows], w2_ref[cols, :])
        o_ref[rows, :] = _rmsnorm(out, g_fin_ref[...]) if final_norm else out


def _resident(shape):
    return pl.BlockSpec(shape, lambda i: (0,) * len(shape), pipeline_mode=pl.Buffered(1))


def _row_slab(rows, cols):
    return pl.BlockSpec((rows, cols), lambda i: (i, 0))


def kernel(x, norm_mix_g, w_in, b_gate, norm_v_g, w_s, b_s, conv_w, w_proj_a, w_proj_b,
           w_out, norm_ff_g, w_ff1, w_ff2, norm_final_g):
    bsz, seq, d = x.shape
    depth = w_in.shape[0]
    assert d == D_MODEL and seq % MIXER_TILE == 0 and seq % FFN_TILE == 0
    tokens = bsz * seq
    bf16 = jnp.bfloat16
    xt = x.reshape(tokens, d)
    params = pltpu.CompilerParams(dimension_semantics=("arbitrary",),
                                  vmem_limit_bytes=VMEM_LIMIT_BYTES)
    hbm = pl.BlockSpec(memory_space=pl.ANY)
    mixer_steps = tokens // MIXER_TILE
    w1_slab = d // mixer_steps
    w2_slab = D_FF // mixer_steps
    assert w1_slab * mixer_steps == d and w1_slab % (2 * SUBLANES) == 0

    for l in range(depth):
        xt, w1_bf16, w2_bf16 = pl.pallas_call(
            functools.partial(_mixer_kernel, layer=l, tiles_per_seq=seq // MIXER_TILE),
            grid=(mixer_steps,),
            in_specs=[
                _row_slab(MIXER_TILE, d),
                _resident((1, d)),
                hbm,
                _resident((1, 2 * d)),
                _resident((1, A_WIDTH)),
                _resident(w_s.shape[1:]),
                _resident((A_GROUPS, SG_BLOCK, 1)),
                _resident(conv_w.shape[1:]),
                hbm,
                hbm,
                hbm,
                _row_slab(w1_slab, D_FF),
                _row_slab(w2_slab, d),
            ],
            out_specs=[
                _row_slab(MIXER_TILE, d),
                _row_slab(w1_slab, D_FF),
                _row_slab(w2_slab, d),
            ],
            out_shape=[
                jax.ShapeDtypeStruct((tokens, d), jnp.float32),
                jax.ShapeDtypeStruct((d, D_FF), bf16),
                jax.ShapeDtypeStruct((D_FF, d), bf16),
            ],
            scratch_shapes=[
                pltpu.VMEM(w_in.shape[1:], bf16),
                pltpu.VMEM(w_proj_a.shape[1:], bf16),
                pltpu.VMEM(w_proj_b.shape[1:], bf16),
                pltpu.VMEM(w_out.shape[1:], bf16),
                pltpu.VMEM((WEIGHT_STAGES, d, WEIGHT_CHUNK), jnp.float32),
                pltpu.SemaphoreType.DMA((WEIGHT_STAGES,)),
                pltpu.VMEM((MIXER_TILE + SUBLANES, B_WIDTH), jnp.float32),
                pltpu.VMEM((MIXER_TILE, A_WIDTH), bf16),
            ],
            compiler_params=params,
            name="mixer",
        )(xt, norm_mix_g[l][None], w_in, b_gate[l][None], norm_v_g[l][None],
          w_s[l], b_s[l][:, :, None], conv_w[l], w_proj_a, w_proj_b, w_out, w_ff1[l], w_ff2[l])

        xt = pl.pallas_call(
            functools.partial(_ffn_kernel, final_norm=(l == depth - 1)),
            grid=(tokens // FFN_TILE,),
            in_specs=[
                _row_slab(FFN_TILE, d),
                _resident((1, d)),
                _resident((d, D_FF)),
                _resident((D_FF, d)),
                _resident((1, d)),
            ],
            out_specs=_row_slab(FFN_TILE, d),
            out_shape=jax.ShapeDtypeStruct((tokens, d), jnp.float32),
            compiler_params=params,
            name="ffn",
        )(xt, norm_ff_g[l][None], w1_bf16, w2_bf16, norm_final_g[None])

    return xt.reshape(bsz, seq, d)
```

```python
import functools

import jax
import jax.numpy as jnp
from jax import lax
from jax.experimental import pallas as pl
from jax.experimental.pallas import tpu as pltpu

D_MODEL = 1024
A_WIDTH = 1024
A_GROUPS = 8
B_WIDTH = 1024
SG_BLOCK = 128
CHUNK = 64
CONV_WIDTH = 3
D_FF = 4 * D_MODEL
EPS = 1e-6

SUBLANES = 8
MIXER_TILE = 512
FFN_TILE = 1024
FF_CHUNK = 1024
FFN_ROW_BLOCK = 256
WEIGHT_CHUNK = 512
WEIGHT_STAGES = 3
VMEM_LIMIT_BYTES = 56 * 1024 * 1024

_U0 = 0
_V0 = A_WIDTH
_BG0 = 2 * A_WIDTH
_CG0 = _BG0 + B_WIDTH
_XS0 = _CG0 + B_WIDTH
_GA0 = _XS0 + B_WIDTH
_GB0 = _GA0 + D_MODEL


def _rmsnorm(x, g):
    return x * lax.rsqrt(jnp.mean(x * x, axis=-1, keepdims=True) + EPS) * g


def _gelu(x):
    return 0.5 * x * (1.0 + lax.erf(x * (2.0 ** -0.5)))


def _sigmoid(x):
    return 0.5 * jnp.tanh(0.5 * x) + 0.5


def _dot(a, b):
    return jnp.dot(a, b, preferred_element_type=jnp.float32)


class _WeightFeed:
    def __init__(self, layer, use_order, stage_ref, sem_ref):
        self.layer, self.stage_ref, self.sem_ref = layer, stage_ref, sem_ref
        self.chunks = [(src, dst, c0 + k) for src, dst, c0, width in use_order
                       for k in range(0, width, WEIGHT_CHUNK)]
        self.position = {(id(dst), c0): i for i, (_, dst, c0) in enumerate(self.chunks)}
        self.started = self.converted = 0
        self._start_more()

    def _copy(self, i):
        src, _, c0 = self.chunks[i]
        slot = i % WEIGHT_STAGES
        return pltpu.make_async_copy(src.at[self.layer, :, pl.ds(c0, WEIGHT_CHUNK)],
                                     self.stage_ref.at[slot], self.sem_ref.at[slot])

    def _start_more(self):
        while self.started < min(len(self.chunks), self.converted + WEIGHT_STAGES):
            self._copy(self.started).start()
            self.started += 1

    def ready(self, dst, c0, width):
        last = self.position[(id(dst), c0 + width - WEIGHT_CHUNK)]
        while self.converted <= last:
            i = self.converted
            _, chunk_dst, chunk_c0 = self.chunks[i]
            self._copy(i).wait()
            chunk_dst[:, chunk_c0:chunk_c0 + WEIGHT_CHUNK] = (
                self.stage_ref[i % WEIGHT_STAGES].astype(jnp.bfloat16))
            self.converted += 1
            self._start_more()

    def finish(self):
        assert self.converted == self.started == len(self.chunks)


def _mixer_kernel(x_ref, g_mix_ref, w_in_hbm, b_gate_ref, g_v_ref, w_s_ref, b_s_ref,
                  conv_w_ref, w_pa_hbm, w_pb_hbm, w_o_hbm, w1_f32_ref, w2_f32_ref,
                  o_ref, w1_bf16_ref, w2_bf16_ref,
                  w_in_ref, w_pa_ref, w_pb_ref, w_o_ref, stage_ref, sem_ref, z_ref, a_ref,
                  *, layer, tiles_per_seq):
    refs = (x_ref, g_mix_ref, b_gate_ref, g_v_ref, w_s_ref, b_s_ref, conv_w_ref, o_ref,
            w_in_ref, w_pa_ref, w_pb_ref, w_o_ref, z_ref, a_ref)

    @pl.when(pl.program_id(0) == 0)
    def _():
        z_ref[0:SUBLANES, :] = jnp.zeros((SUBLANES, B_WIDTH), jnp.float32)
        half = A_WIDTH // 2
        use_order = [(w_in_hbm, w_in_ref, c0 + k * half, half)
                     for heavy0, light0 in ((_V0, _CG0), (_U0, _XS0))
                     for k in range(2) for c0 in (heavy0, light0)]
        use_order += [(w_in_hbm, w_in_ref, _BG0, B_WIDTH), (w_in_hbm, w_in_ref, _GB0, D_MODEL),
                      (w_pb_hbm, w_pb_ref, 0, D_MODEL), (w_in_hbm, w_in_ref, _GA0, D_MODEL),
                      (w_pa_hbm, w_pa_ref, 0, D_MODEL), (w_o_hbm, w_o_ref, 0, D_MODEL)]
        feed = _WeightFeed(layer, use_order, stage_ref, sem_ref)
        _mixer_step(*refs, feed=feed, tiles_per_seq=tiles_per_seq)
        feed.finish()

    @pl.when(pl.program_id(0) != 0)
    def _():
        _mixer_step(*refs, feed=None, tiles_per_seq=tiles_per_seq)

    w1_bf16_ref[...] = w1_f32_ref[...].astype(jnp.bfloat16)
    w2_bf16_ref[...] = w2_f32_ref[...].astype(jnp.bfloat16)


def _mixer_step(x_ref, g_mix_ref, b_gate_ref, g_v_ref, w_s_ref, b_s_ref, conv_w_ref, o_ref,
                w_in_ref, w_pa_ref, w_pb_ref, w_o_ref, z_ref, a_ref, *, feed, tiles_per_seq):
    tm = x_ref.shape[0]
    x = x_ref[...]
    h = _rmsnorm(x, g_mix_ref[...]).astype(jnp.bfloat16)

    def weight(ref, c0, width):
        if feed is not None:
            feed.ready(ref, c0, width)
        return ref[:, c0:c0 + width]

    def proj(c0, width):
        return _dot(h, weight(w_in_ref, c0, width))

    def proj_pair(heavy0, light0):
        half = A_WIDTH // 2
        parts = [proj(c0 + k * half, half) for k in range(2) for c0 in (heavy0, light0)]
        return (jnp.concatenate(parts[0::2], axis=-1), jnp.concatenate(parts[1::2], axis=-1))

    pv, pcg = proj_pair(_V0, _CG0)
    pu, pxs = proj_pair(_U0, _XS0)
    v = _rmsnorm(_gelu(pv), g_v_ref[...]).astype(jnp.bfloat16)
    u = _gelu(pu)

    z = pcg * pxs
    z_ref[SUBLANES:SUBLANES + tm, :] = z
    conv = conv_w_ref[CONV_WIDTH - 1:CONV_WIDTH, :] * z
    for k in range(CONV_WIDTH - 1):
        shift = CONV_WIDTH - 1 - k
        conv = conv + conv_w_ref[k:k + 1, :] * z_ref[SUBLANES - shift:SUBLANES - shift + tm, :]
    next_starts_seq = (pl.program_id(0) + 1) % tiles_per_seq == 0
    z_ref[0:SUBLANES, :] = jnp.where(next_starts_seq, 0.0, z_ref[tm:tm + SUBLANES, :])
    c = (proj(_BG0, B_WIDTH) * conv).astype(jnp.bfloat16)
    gb = proj(_GB0, D_MODEL) + b_gate_ref[:, D_MODEL:2 * D_MODEL]
    m_b = _sigmoid(gb) * _dot(c, weight(w_pb_ref, 0, D_MODEL))

    chunk_row = lax.broadcasted_iota(jnp.int32, (SG_BLOCK, SG_BLOCK), 0) // CHUNK
    chunk_col = lax.broadcasted_iota(jnp.int32, (SG_BLOCK, SG_BLOCK), 1) // CHUNK
    keep = chunk_col <= chunk_row
    for g in range(A_GROUPS):
        w_g = jnp.where(keep, w_s_ref[g], 0.0).astype(jnp.bfloat16)
        b_g = b_s_ref[g]
        cols = slice(g * SG_BLOCK, (g + 1) * SG_BLOCK)
        n_blocks = tm // SG_BLOCK
        v_blocks = jnp.concatenate(
            [v[n * SG_BLOCK:(n + 1) * SG_BLOCK, cols] for n in range(n_blocks)], axis=1)
        mixed = _dot(w_g, v_blocks) + b_g
        for n in range(n_blocks):
            rows = slice(n * SG_BLOCK, (n + 1) * SG_BLOCK)
            a_ref[rows, cols] = (u[rows, cols]
                                 * mixed[:, n * SG_BLOCK:(n + 1) * SG_BLOCK]).astype(jnp.bfloat16)

    ga = proj(_GA0, D_MODEL) + b_gate_ref[:, 0:D_MODEL]
    m = _sigmoid(ga) * _dot(a_ref[...], weight(w_pa_ref, 0, D_MODEL)) + m_b
    o_ref[...] = x + _dot(m.astype(jnp.bfloat16), weight(w_o_ref, 0, D_MODEL))


def _ffn_kernel(x_ref, g_ff_ref, w1_ref, w2_ref, g_fin_ref, o_ref, *, final_norm):
    x = x_ref[...]
    hf = _rmsnorm(x, g_ff_ref[...]).astype(jnp.bfloat16)
    acc = x
    n_chunks = D_FF // FF_CHUNK
    for j in range(n_chunks):
        cols = slice(j * FF_CHUNK, (j + 1) * FF_CHUNK)
        z = jnp.maximum(_dot(hf, w1_ref[:, cols]), 0.0)
        zz = (z * z).astype(jnp.bfloat16)
        if j < n_chunks - 1:
            acc = acc + _dot(zz, w2_ref[cols, :])
    for r in range(x.shape[0] // FFN_ROW_BLOCK):
        rows = slice(r * FFN_ROW_BLOCK, (r + 1) * FFN_ROW_BLOCK)
        out = acc[rows] + _dot(zz[rows], w2_ref[cols, :])
        o_ref[rows, :] = _rmsnorm(out, g_fin_ref[...]) if final_norm else out


def _resident(shape):
    return pl.BlockSpec(shape, lambda i: (0,) * len(shape), pipeline_mode=pl.Buffered(1))


def _row_slab(rows, cols):
    return pl.BlockSpec((rows, cols), lambda i: (i, 0))


def kernel(x, norm_mix_g, w_in, b_gate, norm_v_g, w_s, b_s, conv_w, w_proj_a, w_proj_b,
           w_out, norm_ff_g, w_ff1, w_ff2, norm_final_g):
    bsz, seq, d = x.shape
    depth = w_in.shape[0]
    assert d == D_MODEL and seq % MIXER_TILE == 0 and seq % FFN_TILE == 0
    tokens = bsz * seq
    bf16 = jnp.bfloat16
    xt = x.reshape(tokens, d)
    params = pltpu.CompilerParams(dimension_semantics=("arbitrary",),
                                  vmem_limit_bytes=VMEM_LIMIT_BYTES)
    hbm = pl.BlockSpec(memory_space=pl.ANY)
    mixer_steps = tokens // MIXER_TILE
    w1_slab = d // mixer_steps
    w2_slab = D_FF // mixer_steps
    assert w1_slab * mixer_steps == d and w1_slab % (2 * SUBLANES) == 0

    for l in range(depth):
        xt, w1_bf16, w2_bf16 = pl.pallas_call(
            functools.partial(_mixer_kernel, layer=l, tiles_per_seq=seq // MIXER_TILE),
            grid=(mixer_steps,),
            in_specs=[
                _row_slab(MIXER_TILE, d),
                _resident((1, d)),
                hbm,
                _resident((1, 2 * d)),
                _resident((1, A_WIDTH)),
                _resident(w_s.shape[1:]),
                _resident((A_GROUPS, SG_BLOCK, 1)),
                _resident(conv_w.shape[1:]),
                hbm,
                hbm,
                hbm,
                _row_slab(w1_slab, D_FF),
                _row_slab(w2_slab, d),
            ],
            out_specs=[
                _row_slab(MIXER_TILE, d),
                _row_slab(w1_slab, D_FF),
                _row_slab(w2_slab, d),
            ],
            out_shape=[
                jax.ShapeDtypeStruct((tokens, d), jnp.float32),
                jax.ShapeDtypeStruct((d, D_FF), bf16),
                jax.ShapeDtypeStruct((D_FF, d), bf16),
            ],
            scratch_shapes=[
                pltpu.VMEM(w_in.shape[1:], bf16),
                pltpu.VMEM(w_proj_a.shape[1:], bf16),
                pltpu.VMEM(w_proj_b.shape[1:], bf16),
                pltpu.VMEM(w_out.shape[1:], bf16),
                pltpu.VMEM((WEIGHT_STAGES, d, WEIGHT_CHUNK), jnp.float32),
                pltpu.SemaphoreType.DMA((WEIGHT_STAGES,)),
                pltpu.VMEM((MIXER_TILE + SUBLANES, B_WIDTH), jnp.float32),
                pltpu.VMEM((MIXER_TILE, A_WIDTH), bf16),
            ],
            compiler_params=params,
            name="mixer",
        )(xt, norm_mix_g[l][None], w_in, b_gate[l][None], norm_v_g[l][None],
          w_s[l], b_s[l][:, :, None], conv_w[l], w_proj_a, w_proj_b, w_out, w_ff1[l], w_ff2[l])

        xt = pl.pallas_call(
            functools.partial(_ffn_kernel, final_norm=(l == depth - 1)),
            grid=(tokens // FFN_TILE,),
            in_specs=[
                _row_slab(FFN_TILE, d),
                _resident((1, d)),
                _resident((d, D_FF)),
                _resident((D_FF, d)),
                _resident((1, d)),
            ],
            out_specs=_row_slab(FFN_TILE, d),
            out_shape=jax.ShapeDtypeStruct((tokens, d), jnp.float32),
            compiler_params=params,
            name="ffn",
        )(xt, norm_ff_g[l][None], w1_bf16, w2_bf16, norm_final_g[None])

    return xt.reshape(bsz, seq, d)
```

```python
import functools

import jax
import jax.numpy as jnp
from jax import lax
from jax.experimental import pallas as pl
from jax.experimental.pallas import tpu as pltpu

D_MODEL = 1024
A_WIDTH = 1024
A_GROUPS = 8
B_WIDTH = 1024
SG_BLOCK = 128
CHUNK = 64
CONV_WIDTH = 3
D_FF = 4 * D_MODEL
EPS = 1e-6

SUBLANES = 8
MIXER_TILE = 512
FFN_TILE = 1024
FF_CHUNK = 1024
FFN_ROW_BLOCK = 256
MXU_COLS = 256
WEIGHT_CHUNK = MXU_COLS
WEIGHT_STAGES = 3
VMEM_LIMIT_BYTES = 56 * 1024 * 1024

_U0 = 0
_V0 = A_WIDTH
_BG0 = 2 * A_WIDTH
_CG0 = _BG0 + B_WIDTH
_XS0 = _CG0 + B_WIDTH
_GA0 = _XS0 + B_WIDTH
_GB0 = _GA0 + D_MODEL

_BLOCKS = A_WIDTH // MXU_COLS
_VU0 = 0
_CX0 = _VU0 + 2 * A_WIDTH
_PBG0 = _CX0 + 2 * B_WIDTH
_G0 = _PBG0 + B_WIDTH
_PAIRED_SOURCE_COLUMNS = (
    [src0 + k * MXU_COLS for k in range(_BLOCKS) for src0 in (_V0, _U0)]
    + [src0 + k * MXU_COLS for k in range(_BLOCKS) for src0 in (_CG0, _XS0)]
    + [_BG0 + k * MXU_COLS for k in range(_BLOCKS)]
    + [src0 + k * MXU_COLS for k in range(_BLOCKS) for src0 in (_GA0, _GB0)])


def _rmsnorm(x, g):
    return x * lax.rsqrt(jnp.mean(x * x, axis=-1, keepdims=True) + EPS) * g


def _gelu(x):
    return 0.5 * x * (1.0 + lax.erf(x * (2.0 ** -0.5)))


def _sigmoid(x):
    return 0.5 * jnp.tanh(0.5 * x) + 0.5


def _dot(a, b):
    return jnp.dot(a, b, preferred_element_type=jnp.float32)


def _load_weights_as_bf16(layer, copies, stage_ref, sem_ref):
    chunks = [(src, dst, src_c0, k * WEIGHT_CHUNK)
              for src, dst, src_cols in copies for k, src_c0 in enumerate(src_cols)]

    def copy(i):
        src, _, src_c0, _ = chunks[i]
        slot = i % WEIGHT_STAGES
        return pltpu.make_async_copy(src.at[layer, :, pl.ds(src_c0, WEIGHT_CHUNK)],
                                     stage_ref.at[slot], sem_ref.at[slot])

    for i in range(min(WEIGHT_STAGES, len(chunks))):
        copy(i).start()
    for i, (_, dst, _, dst_c0) in enumerate(chunks):
        copy(i).wait()
        dst[:, dst_c0:dst_c0 + WEIGHT_CHUNK] = stage_ref[i % WEIGHT_STAGES].astype(jnp.bfloat16)
        if i + WEIGHT_STAGES < len(chunks):
            copy(i + WEIGHT_STAGES).start()


def _mixer_kernel(x_ref, g_mix_ref, w_in_hbm, b_gate_ref, g_v_ref, w_s_ref, b_s_ref,
                  conv_w_ref, w_pa_hbm, w_pb_hbm, w_o_hbm, w1_f32_ref, w2_f32_ref,
                  o_ref, w1_bf16_ref, w2_bf16_ref,
                  w_in_ref, w_pa_ref, w_pb_ref, w_o_ref, stage_ref, sem_ref, z_ref, a_ref,
                  *, layer, tiles_per_seq):
    @pl.when(pl.program_id(0) == 0)
    def _():
        z_ref[0:SUBLANES, :] = jnp.zeros((SUBLANES, B_WIDTH), jnp.float32)
        in_order = list(range(0, D_MODEL, WEIGHT_CHUNK))
        _load_weights_as_bf16(layer, [(w_in_hbm, w_in_ref, _PAIRED_SOURCE_COLUMNS),
                                      (w_pb_hbm, w_pb_ref, in_order), (w_pa_hbm, w_pa_ref, in_order),
                                      (w_o_hbm, w_o_ref, in_order)],
                              stage_ref, sem_ref)

    tm = x_ref.shape[0]
    x = x_ref[...]
    h = _rmsnorm(x, g_mix_ref[...]).astype(jnp.bfloat16)

    def proj_pair(c0, k):
        p = _dot(h, w_in_ref[:, c0 + 2 * k * MXU_COLS:c0 + 2 * (k + 1) * MXU_COLS])
        return p[:, :MXU_COLS], p[:, MXU_COLS:]

    def block(ref_or_val, k):
        return ref_or_val[:, k * MXU_COLS:(k + 1) * MXU_COLS]

    gv, gu, z = [], [], []
    for k in range(_BLOCKS):
        pv, pu = proj_pair(_VU0, k)
        gv.append(_gelu(pv))
        gu.append(_gelu(pu))
        pcg, pxs = proj_pair(_CX0, k)
        z.append(pcg * pxs)
    v = _rmsnorm(jnp.concatenate(gv, axis=-1), g_v_ref[...]).astype(jnp.bfloat16)
    u = jnp.concatenate(gu, axis=-1)
    z = jnp.concatenate(z, axis=-1)

    z_ref[SUBLANES:SUBLANES + tm, :] = z
    conv = conv_w_ref[CONV_WIDTH - 1:CONV_WIDTH, :] * z
    for k in range(CONV_WIDTH - 1):
        shift = CONV_WIDTH - 1 - k
        conv = conv + conv_w_ref[k:k + 1, :] * z_ref[SUBLANES - shift:SUBLANES - shift + tm, :]
    next_starts_seq = (pl.program_id(0) + 1) % tiles_per_seq == 0
    z_ref[0:SUBLANES, :] = jnp.where(next_starts_seq, 0.0, z_ref[tm:tm + SUBLANES, :])
    c = (_dot(h, w_in_ref[:, _PBG0:_PBG0 + B_WIDTH]) * conv).astype(jnp.bfloat16)
    c_pb = _dot(c, w_pb_ref[...])

    chunk_row = lax.broadcasted_iota(jnp.int32, (SG_BLOCK, SG_BLOCK), 0) // CHUNK
    chunk_col = lax.broadcasted_iota(jnp.int32, (SG_BLOCK, SG_BLOCK), 1) // CHUNK
    keep = chunk_col <= chunk_row
    for g in range(A_GROUPS):
        w_g = jnp.where(keep, w_s_ref[g], 0.0).astype(jnp.bfloat16)
        b_g = b_s_ref[g]
        cols = slice(g * SG_BLOCK, (g + 1) * SG_BLOCK)
        n_blocks = tm // SG_BLOCK
        v_blocks = jnp.concatenate(
            [v[n * SG_BLOCK:(n + 1) * SG_BLOCK, cols] for n in range(n_blocks)], axis=1)
        mixed = _dot(w_g, v_blocks) + b_g
        for n in range(n_blocks):
            rows = slice(n * SG_BLOCK, (n + 1) * SG_BLOCK)
            a_ref[rows, cols] = (u[rows, cols]
                                 * mixed[:, n * SG_BLOCK:(n + 1) * SG_BLOCK]).astype(jnp.bfloat16)

    a_pa = _dot(a_ref[...], w_pa_ref[...])

    m = []
    for k in range(_BLOCKS):
        ga, gb = proj_pair(_G0, k)
        ga = ga + block(b_gate_ref, k)
        gb = gb + block(b_gate_ref, _BLOCKS + k)
        m.append((_sigmoid(ga) * block(a_pa, k) + _sigmoid(gb) * block(c_pb, k))
                 .astype(jnp.bfloat16))
    o_ref[...] = x + _dot(jnp.concatenate(m, axis=-1), w_o_ref[...])

    w1_bf16_ref[...] = w1_f32_ref[...].astype(jnp.bfloat16)
    w2_bf16_ref[...] = w2_f32_ref[...].astype(jnp.bfloat16)


def _ffn_kernel(x_ref, g_ff_ref, w1_ref, w2_ref, g_fin_ref, o_ref, *, final_norm):
    x = x_ref[...]
    hf = _rmsnorm(x, g_ff_ref[...]).astype(jnp.bfloat16)
    acc = x
    n_chunks = D_FF // FF_CHUNK
    for j in range(n_chunks):
        cols = slice(j * FF_CHUNK, (j + 1) * FF_CHUNK)
        z = jnp.maximum(_dot(hf, w1_ref[:, cols]), 0.0)
        zz = (z * z).astype(jnp.bfloat16)
        if j < n_chunks - 1:
            acc = acc + _dot(zz, w2_ref[cols, :])
    for r in range(x.shape[0] // FFN_ROW_BLOCK):
        rows = slice(r * FFN_ROW_BLOCK, (r + 1) * FFN_ROW_BLOCK)
        out = acc[rows] + _dot(zz[rows], w2_ref[cols, :])
        o_ref[rows, :] = _rmsnorm(out, g_fin_ref[...]) if final_norm else out


def _resident(shape):
    return pl.BlockSpec(shape, lambda i: (0,) * len(shape), pipeline_mode=pl.Buffered(1))


def _row_slab(rows, cols):
    return pl.BlockSpec((rows, cols), lambda i: (i, 0))


def kernel(x, norm_mix_g, w_in, b_gate, norm_v_g, w_s, b_s, conv_w, w_proj_a, w_proj_b,
           w_out, norm_ff_g, w_ff1, w_ff2, norm_final_g):
    bsz, seq, d = x.shape
    depth = w_in.shape[0]
    assert d == D_MODEL and seq % MIXER_TILE == 0 and seq % FFN_TILE == 0
    tokens = bsz * seq
    bf16 = jnp.bfloat16
    xt = x.reshape(tokens, d)
    params = pltpu.CompilerParams(dimension_semantics=("arbitrary",),
                                  vmem_limit_bytes=VMEM_LIMIT_BYTES)
    hbm = pl.BlockSpec(memory_space=pl.ANY)
    mixer_steps = tokens // MIXER_TILE
    w1_slab = d // mixer_steps
    w2_slab = D_FF // mixer_steps
    assert w1_slab * mixer_steps == d and w1_slab % (2 * SUBLANES) == 0

    for l in range(depth):
        xt, w1_bf16, w2_bf16 = pl.pallas_call(
            functools.partial(_mixer_kernel, layer=l, tiles_per_seq=seq // MIXER_TILE),
            grid=(mixer_steps,),
            in_specs=[
                _row_slab(MIXER_TILE, d),
                _resident((1, d)),
                hbm,
                _resident((1, 2 * d)),
                _resident((1, A_WIDTH)),
                _resident(w_s.shape[1:]),
                _resident((A_GROUPS, SG_BLOCK, 1)),
                _resident(conv_w.shape[1:]),
                hbm,
                hbm,
                hbm,
                _row_slab(w1_slab, D_FF),
                _row_slab(w2_slab, d),
            ],
            out_specs=[
                _row_slab(MIXER_TILE, d),
                _row_slab(w1_slab, D_FF),
                _row_slab(w2_slab, d),
            ],
            out_shape=[
                jax.ShapeDtypeStruct((tokens, d), jnp.float32),
                jax.ShapeDtypeStruct((d, D_FF), bf16),
                jax.ShapeDtypeStruct((D_FF, d), bf16),
            ],
            scratch_shapes=[
                pltpu.VMEM(w_in.shape[1:], bf16),
                pltpu.VMEM(w_proj_a.shape[1:], bf16),
                pltpu.VMEM(w_proj_b.shape[1:], bf16),
                pltpu.VMEM(w_out.shape[1:], bf16),
                pltpu.VMEM((WEIGHT_STAGES, d, WEIGHT_CHUNK), jnp.float32),
                pltpu.SemaphoreType.DMA((WEIGHT_STAGES,)),
                pltpu.VMEM((MIXER_TILE + SUBLANES, B_WIDTH), jnp.float32),
                pltpu.VMEM((MIXER_TILE, A_WIDTH), bf16),
            ],
            compiler_params=params,
            name="mixer",
        )(xt, norm_mix_g[l][None], w_in, b_gate[l][None], norm_v_g[l][None],
          w_s[l], b_s[l][:, :, None], conv_w[l], w_proj_a, w_proj_b, w_out, w_ff1[l], w_ff2[l])

        xt = pl.pallas_call(
            functools.partial(_ffn_kernel, final_norm=(l == depth - 1)),
            grid=(tokens // FFN_TILE,),
            in_specs=[
                _row_slab(FFN_TILE, d),
                _resident((1, d)),
                _resident((d, D_FF)),
                _resident((D_FF, d)),
                _resident((1, d)),
            ],
            out_specs=_row_slab(FFN_TILE, d),
            out_shape=jax.ShapeDtypeStruct((tokens, d), jnp.float32),
            compiler_params=params,
            name="ffn",
        )(xt, norm_ff_g[l][None], w1_bf16, w2_bf16, norm_final_g[None])

    return xt.reshape(bsz, seq, d)
```

```python
import functools

import jax
import jax.numpy as jnp
from jax import lax
from jax.experimental import pallas as pl
from jax.experimental.pallas import tpu as pltpu

D_MODEL = 1024
A_WIDTH = 1024
A_GROUPS = 8
B_WIDTH = 1024
SG_BLOCK = 128
CHUNK = 64
CONV_WIDTH = 3
D_FF = 4 * D_MODEL
EPS = 1e-6

SUBLANES = 8
MIXER_TILE = 512
FFN_TILE = 1024
FF_CHUNK = 1024
FFN_ROW_BLOCK = 256
MXU_COLS = 256
WEIGHT_CHUNK = MXU_COLS
WEIGHT_STAGES = 3
VMEM_LIMIT_BYTES = 56 * 1024 * 1024

_U0 = 0
_V0 = A_WIDTH
_BG0 = 2 * A_WIDTH
_CG0 = _BG0 + B_WIDTH
_XS0 = _CG0 + B_WIDTH
_GA0 = _XS0 + B_WIDTH
_GB0 = _GA0 + D_MODEL

_BLOCKS = A_WIDTH // MXU_COLS
_VC0 = 0
_UX0 = _VC0 + A_WIDTH + B_WIDTH
_PBG0 = _UX0 + A_WIDTH + B_WIDTH
_G0 = _PBG0 + B_WIDTH
_PAIRED_SOURCE_COLUMNS = (
    [src0 + k * MXU_COLS for k in range(_BLOCKS) for src0 in (_V0, _CG0)]
    + [src0 + k * MXU_COLS for k in range(_BLOCKS) for src0 in (_U0, _XS0)]
    + [_BG0 + k * MXU_COLS for k in range(_BLOCKS)]
    + [src0 + k * MXU_COLS for k in range(_BLOCKS) for src0 in (_GA0, _GB0)])


def _rmsnorm(x, g):
    return x * lax.rsqrt(jnp.mean(x * x, axis=-1, keepdims=True) + EPS) * g


def _gelu(x):
    return 0.5 * x * (1.0 + lax.erf(x * (2.0 ** -0.5)))


def _sigmoid(x):
    return 0.5 * jnp.tanh(0.5 * x) + 0.5


def _dot(a, b):
    return jnp.dot(a, b, preferred_element_type=jnp.float32)


def _load_weights_as_bf16(layer, copies, stage_ref, sem_ref):
    chunks = [(src, dst, src_c0, k * WEIGHT_CHUNK)
              for src, dst, src_cols in copies for k, src_c0 in enumerate(src_cols)]

    def copy(i):
        src, _, src_c0, _ = chunks[i]
        slot = i % WEIGHT_STAGES
        return pltpu.make_async_copy(src.at[layer, :, pl.ds(src_c0, WEIGHT_CHUNK)],
                                     stage_ref.at[slot], sem_ref.at[slot])

    for i in range(min(WEIGHT_STAGES, len(chunks))):
        copy(i).start()
    for i, (_, dst, _, dst_c0) in enumerate(chunks):
        copy(i).wait()
        dst[:, dst_c0:dst_c0 + WEIGHT_CHUNK] = stage_ref[i % WEIGHT_STAGES].astype(jnp.bfloat16)
        if i + WEIGHT_STAGES < len(chunks):
            copy(i + WEIGHT_STAGES).start()


def _mixer_kernel(x_ref, g_mix_ref, w_in_hbm, b_gate_ref, g_v_ref, w_s_ref, b_s_ref,
                  conv_w_ref, w_pa_hbm, w_pb_hbm, w_o_hbm, w1_f32_ref, w2_f32_ref,
                  o_ref, w1_bf16_ref, w2_bf16_ref,
                  w_in_ref, w_pa_ref, w_pb_ref, w_o_ref, stage_ref, sem_ref, z_ref, a_ref,
                  *, layer, tiles_per_seq):
    @pl.when(pl.program_id(0) == 0)
    def _():
        z_ref[0:SUBLANES, :] = jnp.zeros((SUBLANES, B_WIDTH), jnp.float32)
        in_order = list(range(0, D_MODEL, WEIGHT_CHUNK))
        _load_weights_as_bf16(layer, [(w_in_hbm, w_in_ref, _PAIRED_SOURCE_COLUMNS),
                                      (w_pb_hbm, w_pb_ref, in_order), (w_pa_hbm, w_pa_ref, in_order),
                                      (w_o_hbm, w_o_ref, in_order)],
                              stage_ref, sem_ref)

    tm = x_ref.shape[0]
    x = x_ref[...]
    h = _rmsnorm(x, g_mix_ref[...]).astype(jnp.bfloat16)

    def proj_pair(c0, k):
        p = _dot(h, w_in_ref[:, c0 + 2 * k * MXU_COLS:c0 + 2 * (k + 1) * MXU_COLS])
        return p[:, :MXU_COLS], p[:, MXU_COLS:]

    def block(ref_or_val, k):
        return ref_or_val[:, k * MXU_COLS:(k + 1) * MXU_COLS]

    gv, pcg, gu, pxs = [], [], [], []
    for k in range(_BLOCKS):
        pv_k, pcg_k = proj_pair(_VC0, k)
        gv.append(_gelu(pv_k))
        pcg.append(pcg_k)
    for k in range(_BLOCKS):
        pu_k, pxs_k = proj_pair(_UX0, k)
        gu.append(_gelu(pu_k))
        pxs.append(pxs_k)
    v = _rmsnorm(jnp.concatenate(gv, axis=-1), g_v_ref[...]).astype(jnp.bfloat16)
    u = jnp.concatenate(gu, axis=-1)
    z = jnp.concatenate(pcg, axis=-1) * jnp.concatenate(pxs, axis=-1)

    z_ref[SUBLANES:SUBLANES + tm, :] = z
    conv = conv_w_ref[CONV_WIDTH - 1:CONV_WIDTH, :] * z
    for k in range(CONV_WIDTH - 1):
        shift = CONV_WIDTH - 1 - k
        conv = conv + conv_w_ref[k:k + 1, :] * z_ref[SUBLANES - shift:SUBLANES - shift + tm, :]
    next_starts_seq = (pl.program_id(0) + 1) % tiles_per_seq == 0
    z_ref[0:SUBLANES, :] = jnp.where(next_starts_seq, 0.0, z_ref[tm:tm + SUBLANES, :])
    c = (_dot(h, w_in_ref[:, _PBG0:_PBG0 + B_WIDTH]) * conv).astype(jnp.bfloat16)
    c_pb = _dot(c, w_pb_ref[...])

    chunk_row = lax.broadcasted_iota(jnp.int32, (SG_BLOCK, SG_BLOCK), 0) // CHUNK
    chunk_col = lax.broadcasted_iota(jnp.int32, (SG_BLOCK, SG_BLOCK), 1) // CHUNK
    keep = chunk_col <= chunk_row
    for g in range(A_GROUPS):
        w_g = jnp.where(keep, w_s_ref[g], 0.0).astype(jnp.bfloat16)
        b_g = b_s_ref[g]
        cols = slice(g * SG_BLOCK, (g + 1) * SG_BLOCK)
        n_blocks = tm // SG_BLOCK
        v_blocks = jnp.concatenate(
            [v[n * SG_BLOCK:(n + 1) * SG_BLOCK, cols] for n in range(n_blocks)], axis=1)
        mixed = _dot(w_g, v_blocks) + b_g
        for n in range(n_blocks):
            rows = slice(n * SG_BLOCK, (n + 1) * SG_BLOCK)
            a_ref[rows, cols] = (u[rows, cols]
                                 * mixed[:, n * SG_BLOCK:(n + 1) * SG_BLOCK]).astype(jnp.bfloat16)

    a_pa = _dot(a_ref[...], w_pa_ref[...])

    m = []
    for k in range(_BLOCKS):
        ga, gb = proj_pair(_G0, k)
        ga = ga + block(b_gate_ref, k)
        gb = gb + block(b_gate_ref, _BLOCKS + k)
        m.append((_sigmoid(ga) * block(a_pa, k) + _sigmoid(gb) * block(c_pb, k))
                 .astype(jnp.bfloat16))
    o_ref[...] = x + _dot(jnp.concatenate(m, axis=-1), w_o_ref[...])

    w1_bf16_ref[...] = w1_f32_ref[...].astype(jnp.bfloat16)
    w2_bf16_ref[...] = w2_f32_ref[...].astype(jnp.bfloat16)


def _ffn_kernel(x_ref, g_ff_ref, w1_ref, w2_ref, g_fin_ref, o_ref, *, final_norm):
    x = x_ref[...]
    hf = _rmsnorm(x, g_ff_ref[...]).astype(jnp.bfloat16)
    acc = x
    n_chunks = D_FF // FF_CHUNK
    for j in range(n_chunks):
        cols = slice(j * FF_CHUNK, (j + 1) * FF_CHUNK)
        z = jnp.maximum(_dot(hf, w1_ref[:, cols]), 0.0)
        zz = (z * z).astype(jnp.bfloat16)
        if j < n_chunks - 1:
            acc = acc + _dot(zz, w2_ref[cols, :])
    for r in range(x.shape[0] // FFN_ROW_BLOCK):
        rows = slice(r * FFN_ROW_BLOCK, (r + 1) * FFN_ROW_BLOCK)
        out = acc[rows] + _dot(zz[rows], w2_ref[cols, :])
        o_ref[rows, :] = _rmsnorm(out, g_fin_ref[...]) if final_norm else out


def _resident(shape):
    return pl.BlockSpec(shape, lambda i: (0,) * len(shape), pipeline_mode=pl.Buffered(1))


def _row_slab(rows, cols):
    return pl.BlockSpec((rows, cols), lambda i: (i, 0))


def kernel(x, norm_mix_g, w_in, b_gate, norm_v_g, w_s, b_s, conv_w, w_proj_a, w_proj_b,
           w_out, norm_ff_g, w_ff1, w_ff2, norm_final_g):
    bsz, seq, d = x.shape
    depth = w_in.shape[0]
    assert d == D_MODEL and seq % MIXER_TILE == 0 and seq % FFN_TILE == 0
    tokens = bsz * seq
    bf16 = jnp.bfloat16
    xt = x.reshape(tokens, d)
    params = pltpu.CompilerParams(dimension_semantics=("arbitrary",),
                                  vmem_limit_bytes=VMEM_LIMIT_BYTES)
    hbm = pl.BlockSpec(memory_space=pl.ANY)
    mixer_steps = tokens // MIXER_TILE
    w1_slab = d // mixer_steps
    w2_slab = D_FF // mixer_steps
    assert w1_slab * mixer_steps == d and w1_slab % (2 * SUBLANES) == 0

    for l in range(depth):
        xt, w1_bf16, w2_bf16 = pl.pallas_call(
            functools.partial(_mixer_kernel, layer=l, tiles_per_seq=seq // MIXER_TILE),
            grid=(mixer_steps,),
            in_specs=[
                _row_slab(MIXER_TILE, d),
                _resident((1, d)),
                hbm,
                _resident((1, 2 * d)),
                _resident((1, A_WIDTH)),
                _resident(w_s.shape[1:]),
                _resident((A_GROUPS, SG_BLOCK, 1)),
                _resident(conv_w.shape[1:]),
                hbm,
                hbm,
                hbm,
                _row_slab(w1_slab, D_FF),
                _row_slab(w2_slab, d),
            ],
            out_specs=[
                _row_slab(MIXER_TILE, d),
                _row_slab(w1_slab, D_FF),
                _row_slab(w2_slab, d),
            ],
            out_shape=[
                jax.ShapeDtypeStruct((tokens, d), jnp.float32),
                jax.ShapeDtypeStruct((d, D_FF), bf16),
                jax.ShapeDtypeStruct((D_FF, d), bf16),
            ],
            scratch_shapes=[
                pltpu.VMEM(w_in.shape[1:], bf16),
                pltpu.VMEM(w_proj_a.shape[1:], bf16),
                pltpu.VMEM(w_proj_b.shape[1:], bf16),
                pltpu.VMEM(w_out.shape[1:], bf16),
                pltpu.VMEM((WEIGHT_STAGES, d, WEIGHT_CHUNK), jnp.float32),
                pltpu.SemaphoreType.DMA((WEIGHT_STAGES,)),
                pltpu.VMEM((MIXER_TILE + SUBLANES, B_WIDTH), jnp.float32),
                pltpu.VMEM((MIXER_TILE, A_WIDTH), bf16),
            ],
            compiler_params=params,
            name="mixer",
        )(xt, norm_mix_g[l][None], w_in, b_gate[l][None], norm_v_g[l][None],
          w_s[l], b_s[l][:, :, None], conv_w[l], w_proj_a, w_proj_b, w_out, w_ff1[l], w_ff2[l])

        xt = pl.pallas_call(
            functools.partial(_ffn_kernel, final_norm=(l == depth - 1)),
            grid=(tokens // FFN_TILE,),
            in_specs=[
                _row_slab(FFN_TILE, d),
                _resident((1, d)),
                _resident((d, D_FF)),
                _resident((D_FF, d)),
                _resident((1, d)),
            ],
            out_specs=_row_slab(FFN_TILE, d),
            out_shape=jax.ShapeDtypeStruct((tokens, d), jnp.float32),
            compiler_params=params,
            name="ffn",
        )(xt, norm_ff_g[l][None], w1_bf16, w2_bf16, norm_final_g[None])

    return xt.reshape(bsz, seq, d)
```
